```python
import jax, jax.numpy as jnp
from jax import lax
import numpy as np

D_MODEL = 1024
BATCH = 8
SEQ = 8192
DEPTH = 1

PLE_DIM = 256
ATT_HEADS = 8
ATT_KV_HEADS = 2
HEAD_DIM = 64
WINDOW = 128
BLOCK = 128
D_ATT = ATT_HEADS * HEAD_DIM
D_KV = ATT_KV_HEADS * HEAD_DIM
D_RNN = D_MODEL - D_ATT
RNN_BLOCKS = 8
RNN_BLOCK_DIM = D_RNN // RNN_BLOCKS
RNN_CONV = 4
LRU_C = 8.0
D_MIX = D_ATT + D_RNN
D_IN = D_ATT + 2 * D_KV + 2 * D_RNN
D_FF = 3 * D_MODEL
FFN_CONV = 3
LN_EPS = 1e-5
ALPHA = float((2 * DEPTH) ** 0.25)
BETA = float((8 * DEPTH) ** -0.25)

kernel_name = "hymba_swa_sink_rglru_convglu_deepnorm"


def layer_norm(x, g, b):
    xf = x.astype(jnp.float32)
    mu = jnp.mean(xf, axis=-1, keepdims=True)
    var = jnp.mean(jnp.square(xf - mu), axis=-1, keepdims=True)
    y = (xf - mu) * lax.rsqrt(var + LN_EPS)
    return (y * g.astype(jnp.float32) + b.astype(jnp.float32)).astype(x.dtype)


def causal_dwconv(x, w, b):
    width = w.shape[0]
    y = lax.conv_general_dilated(
        x, w[:, None, :].astype(x.dtype), window_strides=(1,),
        padding=[(width - 1, 0)], dimension_numbers=('NWC', 'WIO', 'NWC'),
        feature_group_count=x.shape[-1])
    return y + b.astype(x.dtype)


def sliding_window_sink_attention(q, k, v, sinks):
    B, S = q.shape[0], q.shape[1]
    nb = S // BLOCK
    grp = ATT_HEADS // ATT_KV_HEADS
    qb = q.reshape(B, nb, BLOCK, ATT_KV_HEADS, grp, HEAD_DIM).astype(jnp.float32)

    def band(t):
        tb = t.reshape(B, nb, BLOCK, ATT_KV_HEADS, HEAD_DIM).astype(jnp.float32)
        prev = jnp.pad(tb, ((0, 0), (1, 0), (0, 0), (0, 0), (0, 0)))[:, :-1]
        return jnp.concatenate([prev, tb], axis=2)

    kb, vb = band(k), band(v)
    scores = jnp.einsum('bnqkgd,bnskd->bnkgqs', qb, kb) * (HEAD_DIM ** -0.5)
    qi = jnp.arange(BLOCK)[:, None]
    sj = jnp.arange(2 * BLOCK)[None, :]
    rel = qi + BLOCK - sj
    in_win = (rel >= 0) & (rel < WINDOW)
    key_ok = (jnp.arange(nb)[:, None] * BLOCK - BLOCK + sj) >= 0
    mask = in_win[None] & key_ok[:, None, :]
    scores = jnp.where(mask[None, :, None, None], scores, -jnp.inf)
    sink = sinks.astype(jnp.float32).reshape(ATT_KV_HEADS, grp)[None, None, :, :, None, None]
    sink = jnp.broadcast_to(sink, scores.shape[:-1] + (1,))
    probs = jax.nn.softmax(jnp.concatenate([scores, sink], axis=-1), axis=-1)[..., :-1]
    out = jnp.einsum('bnkgqs,bnskd->bnqkgd', probs, vb)
    return out.reshape(B, S, D_ATT).astype(q.dtype)


def rg_lru(x, w_a, b_a, w_x, b_x, lam):
    B, S, _ = x.shape
    xf = x.astype(jnp.float32)
    xb = xf.reshape(B, S, RNN_BLOCKS, RNN_BLOCK_DIM)
    r = jax.nn.sigmoid(jnp.einsum('bshi,hij->bshj', xb, w_a.astype(jnp.float32)).reshape(B, S, D_RNN)
                       + b_a.astype(jnp.float32))
    i = jax.nn.sigmoid(jnp.einsum('bshi,hij->bshj', xb, w_x.astype(jnp.float32)).reshape(B, S, D_RNN)
                       + b_x.astype(jnp.float32))
    log_a = -LRU_C * r * jax.nn.softplus(-lam.astype(jnp.float32))
    a = jnp.exp(log_a)
    b = jnp.sqrt(-jnp.expm1(2.0 * log_a)) * (i * xf)

    def combine(c1, c2):
        a1, b1 = c1
        a2, b2 = c2
        return a1 * a2, a2 * b1 + b2

    _, h = lax.associative_scan(combine, (a, b), axis=1)
    return h.astype(x.dtype)


def conv_glu_ffn(h, w_up, conv_w, conv_b, w_down):
    up = h @ w_up
    gate, val = jnp.split(up, 2, axis=-1)
    gate = causal_dwconv(gate, conv_w, conv_b)
    return (jax.nn.gelu(gate, approximate=True) * val) @ w_down


def setup_inputs(seed: int = 0) -> dict:
    key = jax.random.key(seed)
    ks = jax.random.split(key, 24)
    f32 = jnp.float32
    nrm = lambda k, shape, s: jax.random.normal(k, shape, f32) * s
    L = DEPTH
    u = jax.random.uniform(ks[12], (L, D_RNN), f32, 0.9, 0.999)
    s = u ** (1.0 / LRU_C)
    lru_lambda = jnp.log(s) - jnp.log1p(-s)
    return {
        "x": nrm(ks[0], (BATCH, SEQ, D_MODEL), 1.0),
        "p": nrm(ks[1], (DEPTH, BATCH, SEQ, PLE_DIM), 1.0),
        "w_in": nrm(ks[2], (L, D_MODEL, D_IN), D_MODEL ** -0.5),
        "attn_sinks": nrm(ks[3], (L, ATT_HEADS), 0.5),
        "rnn_conv_w": nrm(ks[4], (L, RNN_CONV, D_RNN), RNN_CONV ** -0.5),
        "rnn_conv_b": nrm(ks[5], (L, D_RNN), 0.01),
        "gate_a_w": nrm(ks[6], (L, RNN_BLOCKS, RNN_BLOCK_DIM, RNN_BLOCK_DIM), RNN_BLOCK_DIM ** -0.5),
        "gate_a_b": nrm(ks[7], (L, D_RNN), 0.01),
        "gate_x_w": nrm(ks[8], (L, RNN_BLOCKS, RNN_BLOCK_DIM, RNN_BLOCK_DIM), RNN_BLOCK_DIM ** -0.5),
        "gate_x_b": nrm(ks[9], (L, D_RNN), 0.01),
        "lru_lambda": lru_lambda,
        "w_out": nrm(ks[10], (L, D_MIX, D_MODEL), BETA * D_MIX ** -0.5),
        "ln1_g": 1.0 + nrm(ks[11], (L, D_MODEL), 0.01),
        "ln1_b": nrm(ks[13], (L, D_MODEL), 0.01),
        "w_ffn_up": nrm(ks[14], (L, D_MODEL, 2 * D_FF), D_MODEL ** -0.5),
        "ffn_conv_w": nrm(ks[15], (L, FFN_CONV, D_FF), FFN_CONV ** -0.5),
        "ffn_conv_b": nrm(ks[16], (L, D_FF), 0.01),
        "w_ffn_down": nrm(ks[17], (L, D_FF, D_MODEL), BETA * D_FF ** -0.5),
        "ple_gate_w": nrm(ks[18], (L, D_MODEL, D_MODEL), D_MODEL ** -0.5),
        "ple_gate_b": nrm(ks[19], (L, D_MODEL), 0.01),
        "ple_proj": nrm(ks[20], (L, PLE_DIM, D_MODEL), BETA * PLE_DIM ** -0.5),
        "ln2_g": 1.0 + nrm(ks[21], (L, D_MODEL), 0.01),
        "ln2_b": nrm(ks[22], (L, D_MODEL), 0.01),
    }


def reference(x, p, w_in, attn_sinks, rnn_conv_w, rnn_conv_b, gate_a_w, gate_a_b,
              gate_x_w, gate_x_b, lru_lambda, w_out, ln1_g, ln1_b, w_ffn_up,
              ffn_conv_w, ffn_conv_b, w_ffn_down, ple_gate_w, ple_gate_b, ple_proj,
              ln2_g, ln2_b):
    B, S, _ = x.shape
    splits = [D_ATT, D_ATT + D_KV, D_ATT + 2 * D_KV, D_ATT + 2 * D_KV + D_RNN]
    h = x
    for l in range(DEPTH):
        u = h @ w_in[l]
        q, k, v, xr, gr = jnp.split(u, splits, axis=-1)
        att = sliding_window_sink_attention(
            q.reshape(B, S, ATT_HEADS, HEAD_DIM),
            k.reshape(B, S, ATT_KV_HEADS, HEAD_DIM),
            v.reshape(B, S, ATT_KV_HEADS, HEAD_DIM),
            attn_sinks[l])
        xr = causal_dwconv(xr, rnn_conv_w[l], rnn_conv_b[l])
        rec = rg_lru(xr, gate_a_w[l], gate_a_b[l], gate_x_w[l], gate_x_b[l], lru_lambda[l])
        rec = rec * jax.nn.gelu(gr, approximate=True)
        mix = jnp.concatenate([att, rec], axis=-1) @ w_out[l]
        h = layer_norm(ALPHA * h + mix, ln1_g[l], ln1_b[l])
        ffn = conv_glu_ffn(h, w_ffn_up[l], ffn_conv_w[l], ffn_conv_b[l], w_ffn_down[l])
        ple = jax.nn.sigmoid(h @ ple_gate_w[l] + ple_gate_b[l]) * (p[l] @ ple_proj[l])
        h = layer_norm(ALPHA * h + ffn + ple, ln2_g[l], ln2_b[l])
    return h
```

```python
import functools
import math

import jax
import jax.numpy as jnp
from jax import lax
from jax.experimental import pallas as pl
from jax.experimental.pallas import tpu as pltpu

D_MODEL = 1024
PLE_DIM = 256
ATT_HEADS = 8
ATT_KV_HEADS = 2
HEAD_DIM = 64
WINDOW = 128
BLOCK = 128
D_ATT = ATT_HEADS * HEAD_DIM
D_KV = ATT_KV_HEADS * HEAD_DIM
D_RNN = D_MODEL - D_ATT
RNN_BLOCKS = 8
RNN_BLOCK_DIM = D_RNN // RNN_BLOCKS
RNN_CONV = 4
LRU_C = 8.0
D_IN = D_ATT + 2 * D_KV + 2 * D_RNN
D_FF = 3 * D_MODEL
FFN_CONV = 3
LN_EPS = 1e-5

LANES = 128
SUBLANES = 8
NEG_BIG = -1e30

TILE_MIX = 512
TILE_FFN = 512
FF_CHUNK = 512
VMEM_LIMIT = 56 * 1024 * 1024

_GELU_C = math.sqrt(2.0 / math.pi)


def _gelu_tanh(x):
    return 0.5 * x * (1.0 + jnp.tanh(_GELU_C * (x + 0.044715 * (x * x * x))))


def _sigmoid(x):
    return 0.5 * jnp.tanh(0.5 * x) + 0.5


def _layer_norm(y, g, b):
    mu = jnp.mean(y, axis=-1, keepdims=True)
    d = y - mu
    var = jnp.mean(d * d, axis=-1, keepdims=True)
    return d * lax.rsqrt(var + LN_EPS) * g + b


def _bdot(a, b):
    return jnp.dot(a, b, preferred_element_type=jnp.float32)


def _mixer_kernel(alpha, sinks_ref, x_ref, w_in_ref, w_gate_ref, cw_ref, cb_ref, ba_ref,
                  bx_ref, lam_ref, w_out_ref, g_ref, b_ref, o_ref,
                  q_s, k_s, v_s, xbuf, a_s, b_s, mix_s, hcarry):
    T = TILE_MIX
    bf16 = jnp.bfloat16
    s = pl.program_id(1)

    @pl.when(s == 0)
    def _():
        k_s[:, 0:BLOCK, :] = jnp.zeros((4, BLOCK, LANES), bf16)
        v_s[:, 0:BLOCK, :] = jnp.zeros((4, BLOCK, LANES), bf16)
        xbuf[0:SUBLANES, :] = jnp.zeros((SUBLANES, D_RNN), jnp.float32)
        hcarry[...] = jnp.zeros((SUBLANES, D_RNN), jnp.float32)

    x = x_ref[...]
    u = _bdot(x.astype(bf16), w_in_ref[...])

    q_s[...] = (u[:, 0:D_ATT] * (HEAD_DIM ** -0.5)).astype(bf16)
    lane = lax.broadcasted_iota(jnp.int32, (T, LANES), 1)
    lo = lane < HEAD_DIM
    for buf, col in ((k_s, D_ATT), (v_s, D_ATT + D_KV)):
        t = u[:, col:col + D_KV]
        t_sw = pltpu.roll(t, HEAD_DIM, axis=1)
        zero = jnp.zeros_like(t)
        buf[0, BLOCK:BLOCK + T, :] = jnp.where(lo, t, zero).astype(bf16)
        buf[1, BLOCK:BLOCK + T, :] = jnp.where(lo, zero, t_sw).astype(bf16)
        buf[2, BLOCK:BLOCK + T, :] = jnp.where(lo, t_sw, zero).astype(bf16)
        buf[3, BLOCK:BLOCK + T, :] = jnp.where(lo, zero, t).astype(bf16)

    qi = lax.broadcasted_iota(jnp.int32, (BLOCK, 2 * BLOCK), 0)
    sj = lax.broadcasted_iota(jnp.int32, (BLOCK, 2 * BLOCK), 1)
    rel = qi + BLOCK - sj
    in_win = (rel >= 0) & (rel < WINDOW)
    grp = ATT_HEADS // ATT_KV_HEADS

    def att_block(n, carry):
        r0 = pl.multiple_of(n * BLOCK, BLOCK)
        first = jnp.logical_and(s == 0, n == 0)
        ok = in_win & jnp.logical_or(sj >= BLOCK, jnp.logical_not(first))
        bias = jnp.where(ok, 0.0, NEG_BIG)
        for pair in range(ATT_HEADS // 2):
            q2 = q_s[pl.ds(r0, BLOCK), pair * LANES:(pair + 1) * LANES]
            acc = None
            for half in range(2):
                h = 2 * pair + half
                var = 2 * (h // grp) + half
                keys = k_s[var, pl.ds(r0, 2 * BLOCK), :]
                vals = v_s[var, pl.ds(r0, 2 * BLOCK), :]
                sc = lax.dot_general(q2, keys, (((1,), (1,)), ((), ())),
                                     preferred_element_type=jnp.float32) + bias
                sink = sinks_ref[h]
                m = jnp.maximum(jnp.max(sc, axis=-1, keepdims=True), sink)
                p = jnp.exp(sc - m)
                l = jnp.sum(p, axis=-1, keepdims=True) + jnp.exp(sink - m)
                o = _bdot(p.astype(bf16), vals) * (1.0 / l)
                acc = o if acc is None else acc + o
            mix_s[pl.ds(r0, BLOCK), pair * LANES:(pair + 1) * LANES] = acc.astype(bf16)
        return carry

    lax.fori_loop(0, T // BLOCK, att_block, 0)
    k_s[:, 0:BLOCK, :] = k_s[:, T:T + BLOCK, :]
    v_s[:, 0:BLOCK, :] = v_s[:, T:T + BLOCK, :]

    c0 = D_ATT + 2 * D_KV
    xr = u[:, c0:c0 + D_RNN]
    gr = u[:, c0 + D_RNN:c0 + 2 * D_RNN]
    xbuf[SUBLANES:SUBLANES + T, :] = xr
    xc = cb_ref[...] + cw_ref[RNN_CONV - 1:RNN_CONV, :] * xr
    for j in range(1, RNN_CONV):
        xc = xc + cw_ref[RNN_CONV - 1 - j:RNN_CONV - j, :] * xbuf[SUBLANES - j:SUBLANES - j + T, :]
    xbuf[0:SUBLANES, :] = xbuf[T:T + SUBLANES, :]

    xcb = xc.astype(bf16)
    ga, gx = [], []
    for c in range(D_RNN // LANES):
        gc = _bdot(xcb[:, c * LANES:(c + 1) * LANES], w_gate_ref[c])
        ga.append(gc[:, 0:LANES])
        gx.append(gc[:, LANES:2 * LANES])
    r = _sigmoid(jnp.concatenate(ga, axis=1) + ba_ref[...])
    i = _sigmoid(jnp.concatenate(gx, axis=1) + bx_ref[...])
    z = -lam_ref[...]
    softplus = jnp.maximum(z, 0.0) + jnp.log1p(jnp.exp(-jnp.abs(z)))
    log_a = (-LRU_C * softplus) * r
    a = jnp.exp(log_a)
    bb = jnp.sqrt(-jnp.tanh(log_a) * (a * a + 1.0)) * (i * xc)

    row = lax.broadcasted_iota(jnp.int32, (T, D_RNN), 0) & (SUBLANES - 1)
    d = 1
    while d < SUBLANES:
        keep = row >= d
        a_sh = jnp.where(keep, pltpu.roll(a, d, axis=0), 1.0)
        b_sh = jnp.where(keep, pltpu.roll(bb, d, axis=0), 0.0)
        bb = a * b_sh + bb
        a = a * a_sh
        d *= 2
    a_s[...] = a
    b_s[...] = bb

    def group(gi, hc):
        r0 = pl.multiple_of(gi * SUBLANES, SUBLANES)
        h = a_s[pl.ds(r0, SUBLANES), :] * hc + b_s[pl.ds(r0, SUBLANES), :]
        b_s[pl.ds(r0, SUBLANES), :] = h
        return jnp.broadcast_to(h[SUBLANES - 1:SUBLANES, :], (SUBLANES, D_RNN))

    hcarry[...] = lax.fori_loop(0, T // SUBLANES, group, hcarry[...], unroll=8)
    mix_s[:, D_ATT:D_ATT + D_RNN] = (b_s[...] * _gelu_tanh(gr)).astype(bf16)

    y = alpha * x + _bdot(mix_s[...], w_out_ref[...])
    o_ref[...] = _layer_norm(y, g_ref[...], b_ref[...])


def _const_spec(shape):
    nd = len(shape)
    return pl.BlockSpec(shape, lambda b, s: (0,) * nd, pipeline_mode=pl.Buffered(1))


def _mixer(x, alpha, sinks, w_in, w_gate, cw, cb, ba, bx, lam, w_out, g, b):
    B, S, _ = x.shape
    T = TILE_MIX
    assert S % T == 0 and T % BLOCK == 0
    row = lambda v: v.reshape(1, -1).astype(jnp.float32)
    tok = pl.BlockSpec((None, T, D_MODEL), lambda b, s: (b, s, 0))
    return pl.pallas_call(
        functools.partial(_mixer_kernel, alpha),
        grid=(B, S // T),
        in_specs=[
            pl.BlockSpec(memory_space=pltpu.SMEM),
            tok,
            _const_spec((D_MODEL, D_IN)),
            _const_spec((D_RNN // LANES, LANES, 2 * LANES)),
            _const_spec((RNN_CONV, D_RNN)),
            _const_spec((1, D_RNN)),
            _const_spec((1, D_RNN)),
            _const_spec((1, D_RNN)),
            _const_spec((1, D_RNN)),
            _const_spec((D_MODEL, D_MODEL)),
            _const_spec((1, D_MODEL)),
            _const_spec((1, D_MODEL)),
        ],
        out_specs=tok,
        out_shape=jax.ShapeDtypeStruct((B, S, D_MODEL), jnp.float32),
        scratch_shapes=[
            pltpu.VMEM((T, D_ATT), jnp.bfloat16),
            pltpu.VMEM((4, T + BLOCK, LANES), jnp.bfloat16),
            pltpu.VMEM((4, T + BLOCK, LANES), jnp.bfloat16),
            pltpu.VMEM((T + SUBLANES, D_RNN), jnp.float32),
            pltpu.VMEM((T, D_RNN), jnp.float32),
            pltpu.VMEM((T, D_RNN), jnp.float32),
            pltpu.VMEM((T, D_MODEL), jnp.bfloat16),
            pltpu.VMEM((SUBLANES, D_RNN), jnp.float32),
        ],
        compiler_params=pltpu.CompilerParams(
            dimension_semantics=("arbitrary", "arbitrary"), vmem_limit_bytes=VMEM_LIMIT),
        name="mixer",
    )(sinks.astype(jnp.float32), x, w_in, w_gate, cw.astype(jnp.float32), row(cb), row(ba),
      row(bx), row(lam), w_out, row(g), row(b))


def _ffn_kernel(alpha, h_ref, p_ref, wup_ref, cw_ref, cb_ref, wdn_ref, wg_ref, bg_ref, wp_ref,
                g_ref, b_ref, o_ref, gbuf, gcarry, acc):
    T = TILE_FFN
    C = FF_CHUNK
    bf16 = jnp.bfloat16
    s = pl.program_id(1)

    @pl.when(s == 0)
    def _():
        gcarry[...] = jnp.zeros((SUBLANES, D_FF), jnp.float32)

    h = h_ref[...]
    hb = h.astype(bf16)
    ple = _sigmoid(_bdot(hb, wg_ref[...]) + bg_ref[...]) * _bdot(p_ref[...].astype(bf16), wp_ref[...])
    acc[...] = alpha * h + ple
    for c in range(D_FF // C):
        lo, hi = c * C, (c + 1) * C
        gate = _bdot(hb, wup_ref[:, lo:hi])
        val = _bdot(hb, wup_ref[:, D_FF + lo:D_FF + hi])
        gbuf[0:SUBLANES, :] = gcarry[:, lo:hi]
        gbuf[SUBLANES:SUBLANES + T, :] = gate
        gcarry[:, lo:hi] = gate[T - SUBLANES:T, :]
        gc = cb_ref[:, lo:hi] + cw_ref[FFN_CONV - 1:FFN_CONV, lo:hi] * gate
        for j in range(1, FFN_CONV):
            gc = gc + cw_ref[FFN_CONV - 1 - j:FFN_CONV - j, lo:hi] * gbuf[SUBLANES - j:SUBLANES - j + T, :]
        act = (_gelu_tanh(gc) * val).astype(bf16)
        acc[...] += _bdot(act, wdn_ref[lo:hi, :])
    o_ref[...] = _layer_norm(acc[...], g_ref[...], b_ref[...])


def _ffn(h, p, alpha, wup, cw, cb, wdn, wg, bg, wp, g, b):
    B, S, _ = h.shape
    T = TILE_FFN
    assert S % T == 0 and D_FF % FF_CHUNK == 0
    row = lambda v: v.reshape(1, -1).astype(jnp.float32)
    tok = pl.BlockSpec((None, T, D_MODEL), lambda b, s: (b, s, 0))
    return pl.pallas_call(
        functools.partial(_ffn_kernel, alpha),
        grid=(B, S // T),
        in_specs=[
            tok,
            pl.BlockSpec((None, T, PLE_DIM), lambda b, s: (b, s, 0)),
            _const_spec((D_MODEL, 2 * D_FF)),
            _const_spec((FFN_CONV, D_FF)),
            _const_spec((1, D_FF)),
            _const_spec((D_FF, D_MODEL)),
            _const_spec((D_MODEL, D_MODEL)),
            _const_spec((1, D_MODEL)),
            _const_spec((PLE_DIM, D_MODEL)),
            _const_spec((1, D_MODEL)),
            _const_spec((1, D_MODEL)),
        ],
        out_specs=tok,
        out_shape=jax.ShapeDtypeStruct((B, S, D_MODEL), jnp.float32),
        scratch_shapes=[
            pltpu.VMEM((T + SUBLANES, FF_CHUNK), jnp.float32),
            pltpu.VMEM((SUBLANES, D_FF), jnp.float32),
            pltpu.VMEM((T, D_MODEL), jnp.float32),
        ],
        compiler_params=pltpu.CompilerParams(
            dimension_semantics=("arbitrary", "arbitrary"), vmem_limit_bytes=VMEM_LIMIT),
        name="ffn",
    )(h, p, wup, cw.astype(jnp.float32), row(cb), wdn, wg, row(bg), wp, row(g), row(b))


def _pack_gate_weights(wa, wx):
    def bd(w):
        w = w.reshape(D_RNN // LANES, 2, RNN_BLOCK_DIM, RNN_BLOCK_DIM)
        z = jnp.zeros_like(w[:, 0])
        top = jnp.concatenate([w[:, 0], z], axis=2)
        bot = jnp.concatenate([z, w[:, 1]], axis=2)
        return jnp.concatenate([top, bot], axis=1)
    return jnp.concatenate([bd(wa), bd(wx)], axis=2).astype(jnp.bfloat16)


def kernel(x, p, w_in, attn_sinks, rnn_conv_w, rnn_conv_b, gate_a_w, gate_a_b, gate_x_w, gate_x_b,
           lru_lambda, w_out, ln1_g, ln1_b, w_ffn_up, ffn_conv_w, ffn_conv_b, w_ffn_down,
           ple_gate_w, ple_gate_b, ple_proj, ln2_g, ln2_b):
    depth = w_in.shape[0]
    alpha = float((2 * depth) ** 0.25)
    bf16 = jnp.bfloat16
    h = x
    for l in range(depth):
        h = _mixer(h, alpha, attn_sinks[l], w_in[l].astype(bf16),
                   _pack_gate_weights(gate_a_w[l], gate_x_w[l]), rnn_conv_w[l], rnn_conv_b[l],
                   gate_a_b[l], gate_x_b[l], lru_lambda[l], w_out[l].astype(bf16), ln1_g[l], ln1_b[l])
        h = _ffn(h, p[l], alpha, w_ffn_up[l].astype(bf16), ffn_conv_w[l], ffn_conv_b[l],
                 w_ffn_down[l].astype(bf16), ple_gate_w[l].astype(bf16), ple_gate_b[l],
                 ple_proj[l].astype(bf16), ln2_g[l], ln2_b[l])
    return h
```

```python
import functools
import math

import jax
import jax.numpy as jnp
from jax import lax
from jax.experimental import pallas as pl
from jax.experimental.pallas import tpu as pltpu

D_MODEL = 1024
PLE_DIM = 256
ATT_HEADS = 8
ATT_KV_HEADS = 2
HEAD_DIM = 64
WINDOW = 128
BLOCK = 128
D_ATT = ATT_HEADS * HEAD_DIM
D_KV = ATT_KV_HEADS * HEAD_DIM
D_RNN = D_MODEL - D_ATT
RNN_BLOCKS = 8
RNN_BLOCK_DIM = D_RNN // RNN_BLOCKS
RNN_CONV = 4
LRU_C = 8.0
D_IN = D_ATT + 2 * D_KV + 2 * D_RNN
D_FF = 3 * D_MODEL
FFN_CONV = 3
LN_EPS = 1e-5

LANES = 128
SUBLANES = 8
NEG_BIG = -1e30

TILE_MIX = 512
TILE_FFN = 512
FF_CHUNK = 512
VMEM_LIMIT = 56 * 1024 * 1024

_GELU_C = math.sqrt(2.0 / math.pi)
_LOG2E = 1.4426950408889634


def _gelu_tanh(x):
    return 0.5 * x * (1.0 + jnp.tanh(_GELU_C * (x + 0.044715 * (x * x * x))))


def _sigmoid(x):
    return 0.5 * jnp.tanh(0.5 * x) + 0.5


def _layer_norm(y, g, b):
    mu = jnp.mean(y, axis=-1, keepdims=True)
    d = y - mu
    var = jnp.mean(d * d, axis=-1, keepdims=True)
    return d * lax.rsqrt(var + LN_EPS) * g + b


def _bdot(a, b):
    return jnp.dot(a, b, preferred_element_type=jnp.float32)


def _w(ref_slice):
    return pltpu.bitcast(ref_slice, jnp.bfloat16)


def _mixer_kernel(alpha, sinks_ref, x_ref, w_in_ref, w_gate_ref, cw_ref, cb_ref, ba_ref,
                  bx_ref, lam_ref, w_out_ref, g_ref, b_ref, o_ref,
                  q_s, k_s, v_s, xbuf, qbuf, a_s, b_s, mix_s, hcarry):
    T = TILE_MIX
    bf16 = jnp.bfloat16
    s = pl.program_id(1)

    @pl.when(s == 0)
    def _():
        k_s[:, 0:BLOCK, :] = jnp.zeros((4, BLOCK, LANES), bf16)
        v_s[:, 0:BLOCK, :] = jnp.zeros((4, BLOCK, LANES), bf16)
        xbuf[0:SUBLANES, :] = jnp.zeros((SUBLANES, D_RNN), jnp.float32)
        qbuf[0:SUBLANES, :] = jnp.zeros((SUBLANES, D_RNN), jnp.float32)
        hcarry[...] = jnp.zeros((SUBLANES, D_RNN), jnp.float32)

    x = x_ref[...]
    u = _bdot(x.astype(bf16), _w(w_in_ref[...]))

    q_s[...] = (u[:, 0:D_ATT] * (HEAD_DIM ** -0.5)).astype(bf16)
    lane = lax.broadcasted_iota(jnp.int32, (T, LANES), 1)
    lo = lane < HEAD_DIM
    for buf, col in ((k_s, D_ATT), (v_s, D_ATT + D_KV)):
        t = u[:, col:col + D_KV]
        t_sw = pltpu.roll(t, HEAD_DIM, axis=1)
        zero = jnp.zeros_like(t)
        buf[0, BLOCK:BLOCK + T, :] = jnp.where(lo, t, zero).astype(bf16)
        buf[1, BLOCK:BLOCK + T, :] = jnp.where(lo, zero, t_sw).astype(bf16)
        buf[2, BLOCK:BLOCK + T, :] = jnp.where(lo, t_sw, zero).astype(bf16)
        buf[3, BLOCK:BLOCK + T, :] = jnp.where(lo, zero, t).astype(bf16)

    qi = lax.broadcasted_iota(jnp.int32, (BLOCK, BLOCK), 0)
    kc = lax.broadcasted_iota(jnp.int32, (BLOCK, BLOCK), 1)
    cur = kc <= qi
    grp = ATT_HEADS // ATT_KV_HEADS

    for n in range(T // BLOCK):
        r0 = n * BLOCK
        prev_bias = jnp.where(s == 0, NEG_BIG, 0.0)
        for pair in range(ATT_HEADS // 2):
            q2 = q_s[pl.ds(r0, BLOCK), pair * LANES:(pair + 1) * LANES]
            acc = None
            for half in range(2):
                h = 2 * pair + half
                var = 2 * (h // grp) + half
                keys = k_s[var, pl.ds(r0, 2 * BLOCK), :]
                vals = v_s[var, pl.ds(r0, 2 * BLOCK), :]
                sc = lax.dot_general(q2, keys, (((1,), (1,)), ((), ())),
                                     preferred_element_type=jnp.float32)
                prev = sc[:, :BLOCK] + prev_bias if n == 0 else sc[:, :BLOCK]
                sc = jnp.where(cur, sc[:, BLOCK:], prev)
                sink = sinks_ref[h]
                m = jnp.maximum(jnp.max(sc, axis=-1, keepdims=True), sink)
                p = jnp.exp(sc - m)
                l = jnp.sum(p, axis=-1, keepdims=True) + jnp.exp(sink - m)
                p2 = jnp.concatenate([jnp.where(cur, 0.0, p), jnp.where(cur, p, 0.0)], axis=1)
                o = _bdot(p2.astype(bf16), vals) * (1.0 / l)
                acc = o if acc is None else acc + o
            mix_s[pl.ds(r0, BLOCK), pair * LANES:(pair + 1) * LANES] = acc.astype(bf16)
    k_s[:, 0:BLOCK, :] = k_s[:, T:T + BLOCK, :]
    v_s[:, 0:BLOCK, :] = v_s[:, T:T + BLOCK, :]

    c0 = D_ATT + 2 * D_KV
    xr = u[:, c0:c0 + D_RNN]
    gr = u[:, c0 + D_RNN:c0 + 2 * D_RNN]
    hw = 0.5 * cw_ref[...]
    xbuf[SUBLANES:SUBLANES + T, :] = xr
    x1 = xbuf[SUBLANES - 1:SUBLANES - 1 + T, :]
    xbuf[0:SUBLANES, :] = xbuf[T:T + SUBLANES, :]
    qbuf[SUBLANES:SUBLANES + T, :] = hw[1:2, :] * xr + hw[0:1, :] * x1
    xh = (0.5 * cb_ref[...] + hw[3:4, :] * xr) + hw[2:3, :] * x1 + qbuf[SUBLANES - 2:SUBLANES - 2 + T, :]
    qbuf[0:SUBLANES, :] = qbuf[T:T + SUBLANES, :]

    xhb = xh.astype(bf16)
    ga, gx = [], []
    for c in range(D_RNN // LANES):
        gc = _bdot(xhb[:, c * LANES:(c + 1) * LANES], _w(w_gate_ref[c]))
        ga.append(gc[:, 0:LANES])
        gx.append(gc[:, LANES:2 * LANES])
    th_r = jnp.tanh(jnp.concatenate(ga, axis=1) + 0.5 * ba_ref[...])
    th_i = jnp.tanh(jnp.concatenate(gx, axis=1) + 0.5 * bx_ref[...])
    z = -lam_ref[...]
    softplus = jnp.maximum(z, 0.0) + jnp.log1p(jnp.exp(-jnp.abs(z)))
    nk = (0.5 * LRU_C) * softplus
    nla = nk * th_r + nk
    a = jnp.exp2(nla * (-_LOG2E))
    y = jnp.tanh(nla) * (a * a + 1.0)
    root = jnp.where(y > 0.0, y * lax.rsqrt(y), 0.0)
    bb = root * (xh * th_i + xh)

    row = lax.broadcasted_iota(jnp.int32, (T, D_RNN), 0) & (SUBLANES - 1)
    d = 1
    while d < SUBLANES:
        keep = row >= d
        a_sh = jnp.where(keep, pltpu.roll(a, d, axis=0), 1.0)
        b_sh = jnp.where(keep, pltpu.roll(bb, d, axis=0), 0.0)
        bb = a * b_sh + bb
        a = a * a_sh
        d *= 2
    a_s[...] = a
    b_s[...] = bb

    hc = hcarry[...]
    for gi in range(T // SUBLANES):
        r0 = gi * SUBLANES
        h = a_s[r0:r0 + SUBLANES, :] * hc + b_s[r0:r0 + SUBLANES, :]
        b_s[r0:r0 + SUBLANES, :] = h
        hc = jnp.broadcast_to(h[SUBLANES - 1:SUBLANES, :], (SUBLANES, D_RNN))
    hcarry[...] = hc
    mix_s[:, D_ATT:D_ATT + D_RNN] = (b_s[...] * _gelu_tanh(gr)).astype(bf16)

    y = alpha * x + _bdot(mix_s[...], _w(w_out_ref[...]))
    o_ref[...] = _layer_norm(y, g_ref[...], b_ref[...])


def _const_spec(shape):
    nd = len(shape)
    return pl.BlockSpec(shape, lambda b, s: (0,) * nd, pipeline_mode=pl.Buffered(1))


def _mixer(x, alpha, sinks, w_in, w_gate, cw, cb, ba, bx, lam, w_out, g, b):
    B, S, _ = x.shape
    T = TILE_MIX
    assert S % T == 0 and T % BLOCK == 0
    row = lambda v: v.reshape(1, -1).astype(jnp.float32)
    tok = pl.BlockSpec((None, T, D_MODEL), lambda b, s: (b, s, 0))
    return pl.pallas_call(
        functools.partial(_mixer_kernel, alpha),
        grid=(B, S // T),
        in_specs=[
            pl.BlockSpec(memory_space=pltpu.SMEM),
            tok,
            _const_spec((D_MODEL // 2, D_IN)),
            _const_spec((D_RNN // LANES, LANES // 2, 2 * LANES)),
            _const_spec((RNN_CONV, D_RNN)),
            _const_spec((1, D_RNN)),
            _const_spec((1, D_RNN)),
            _const_spec((1, D_RNN)),
            _const_spec((1, D_RNN)),
            _const_spec((D_MODEL // 2, D_MODEL)),
            _const_spec((1, D_MODEL)),
            _const_spec((1, D_MODEL)),
        ],
        out_specs=tok,
        out_shape=jax.ShapeDtypeStruct((B, S, D_MODEL), jnp.float32),
        scratch_shapes=[
            pltpu.VMEM((T, D_ATT), jnp.bfloat16),
            pltpu.VMEM((4, T + BLOCK, LANES), jnp.bfloat16),
            pltpu.VMEM((4, T + BLOCK, LANES), jnp.bfloat16),
            pltpu.VMEM((T + SUBLANES, D_RNN), jnp.float32),
            pltpu.VMEM((T + SUBLANES, D_RNN), jnp.float32),
            pltpu.VMEM((T, D_RNN), jnp.float32),
            pltpu.VMEM((T, D_RNN), jnp.float32),
            pltpu.VMEM((T, D_MODEL), jnp.bfloat16),
            pltpu.VMEM((SUBLANES, D_RNN), jnp.float32),
        ],
        compiler_params=pltpu.CompilerParams(
            dimension_semantics=("arbitrary", "arbitrary"), vmem_limit_bytes=VMEM_LIMIT),
        name="mixer",
    )(sinks.astype(jnp.float32), x, w_in, w_gate, cw.astype(jnp.float32), row(cb), row(ba),
      row(bx), row(lam), w_out, row(g), row(b))


def _ffn_kernel(alpha, h_ref, p_ref, wup_ref, cw_ref, cb_ref, wdn_ref, wg_ref, bg_ref, wp_ref,
                g_ref, b_ref, o_ref, gbuf, gcarry, acc):
    T = TILE_FFN
    C = FF_CHUNK
    bf16 = jnp.bfloat16
    s = pl.program_id(1)

    @pl.when(s == 0)
    def _():
        gcarry[...] = jnp.zeros((SUBLANES, D_FF), jnp.float32)

    h = h_ref[...]
    hb = h.astype(bf16)
    ple = _sigmoid(_bdot(hb, _w(wg_ref[...])) + bg_ref[...]) * _bdot(p_ref[...].astype(bf16), _w(wp_ref[...]))
    acc[...] = alpha * h + ple
    for c in range(D_FF // C):
        lo, hi = c * C, (c + 1) * C
        gate = _bdot(hb, _w(wup_ref[:, lo:hi]))
        val = _bdot(hb, _w(wup_ref[:, D_FF + lo:D_FF + hi]))
        gbuf[0:SUBLANES, :] = gcarry[:, lo:hi]
        gbuf[SUBLANES:SUBLANES + T, :] = gate
        gcarry[:, lo:hi] = gate[T - SUBLANES:T, :]
        gc = cb_ref[:, lo:hi] + cw_ref[FFN_CONV - 1:FFN_CONV, lo:hi] * gate
        for j in range(1, FFN_CONV):
            gc = gc + cw_ref[FFN_CONV - 1 - j:FFN_CONV - j, lo:hi] * gbuf[SUBLANES - j:SUBLANES - j + T, :]
        act = (_gelu_tanh(gc) * val).astype(bf16)
        acc[...] += _bdot(act, _w(wdn_ref[lo // 2:hi // 2, :]))
    o_ref[...] = _layer_norm(acc[...], g_ref[...], b_ref[...])


def _ffn(h, p, alpha, wup, cw, cb, wdn, wg, bg, wp, g, b):
    B, S, _ = h.shape
    T = TILE_FFN
    assert S % T == 0 and D_FF % FF_CHUNK == 0
    row = lambda v: v.reshape(1, -1).astype(jnp.float32)
    tok = pl.BlockSpec((None, T, D_MODEL), lambda b, s: (b, s, 0))
    return pl.pallas_call(
        functools.partial(_ffn_kernel, alpha),
        grid=(B, S // T),
        in_specs=[
            tok,
            pl.BlockSpec((None, T, PLE_DIM), lambda b, s: (b, s, 0)),
            _const_spec((D_MODEL // 2, 2 * D_FF)),
            _const_spec((FFN_CONV, D_FF)),
            _const_spec((1, D_FF)),
            _const_spec((D_FF // 2, D_MODEL)),
            _const_spec((D_MODEL // 2, D_MODEL)),
            _const_spec((1, D_MODEL)),
            _const_spec((PLE_DIM // 2, D_MODEL)),
            _const_spec((1, D_MODEL)),
            _const_spec((1, D_MODEL)),
        ],
        out_specs=tok,
        out_shape=jax.ShapeDtypeStruct((B, S, D_MODEL), jnp.float32),
        scratch_shapes=[
            pltpu.VMEM((T + SUBLANES, FF_CHUNK), jnp.float32),
            pltpu.VMEM((SUBLANES, D_FF), jnp.float32),
            pltpu.VMEM((T, D_MODEL), jnp.float32),
        ],
        compiler_params=pltpu.CompilerParams(
            dimension_semantics=("arbitrary", "arbitrary"), vmem_limit_bytes=VMEM_LIMIT),
        name="ffn",
    )(h, p, wup, cw.astype(jnp.float32), row(cb), wdn, wg, row(bg), wp, row(g), row(b))


def _pack_rows(w):
    wb = w.astype(jnp.bfloat16)
    *lead, k, n = wb.shape
    pairs = jnp.swapaxes(wb.reshape(*lead, k // 2, 2, n), -1, -2)
    return lax.bitcast_convert_type(pairs, jnp.uint32)


def _gate_block_diag(wa, wx):
    def bd(w):
        w = w.reshape(D_RNN // LANES, 2, RNN_BLOCK_DIM, RNN_BLOCK_DIM)
        z = jnp.zeros_like(w[:, 0])
        top = jnp.concatenate([w[:, 0], z], axis=2)
        bot = jnp.concatenate([z, w[:, 1]], axis=2)
        return jnp.concatenate([top, bot], axis=1)
    return jnp.concatenate([bd(wa), bd(wx)], axis=2)


def kernel(x, p, w_in, attn_sinks, rnn_conv_w, rnn_conv_b, gate_a_w, gate_a_b, gate_x_w, gate_x_b,
           lru_lambda, w_out, ln1_g, ln1_b, w_ffn_up, ffn_conv_w, ffn_conv_b, w_ffn_down,
           ple_gate_w, ple_gate_b, ple_proj, ln2_g, ln2_b):
    depth = w_in.shape[0]
    alpha = float((2 * depth) ** 0.25)
    h = x
    for l in range(depth):
        h = _mixer(h, alpha, attn_sinks[l], _pack_rows(w_in[l]),
                   _pack_rows(_gate_block_diag(gate_a_w[l], gate_x_w[l])), rnn_conv_w[l], rnn_conv_b[l],
                   gate_a_b[l], gate_x_b[l], lru_lambda[l], _pack_rows(w_out[l]), ln1_g[l], ln1_b[l])
        h = _ffn(h, p[l], alpha, _pack_rows(w_ffn_up[l]), ffn_conv_w[l], ffn_conv_b[l],
                 _pack_rows(w_ffn_down[l]), _pack_rows(ple_gate_w[l]), ple_gate_b[l],
                 _pack_rows(ple_proj[l]), ln2_g[l], ln2_b[l])
    return h
```

```python
import functools
import math

import jax
import jax.numpy as jnp
from jax import lax
from jax.experimental import pallas as pl
from jax.experimental.pallas import tpu as pltpu

D_MODEL = 1024
PLE_DIM = 256
ATT_HEADS = 8
ATT_KV_HEADS = 2
HEAD_DIM = 64
WINDOW = 128
BLOCK = 128
D_ATT = ATT_HEADS * HEAD_DIM
D_KV = ATT_KV_HEADS * HEAD_DIM
D_QKV = D_ATT + 2 * D_KV
D_RNN = D_MODEL - D_ATT
RNN_BLOCKS = 8
RNN_BLOCK_DIM = D_RNN // RNN_BLOCKS
RNN_CONV = 4
LRU_C = 8.0
D_IN = D_QKV + 2 * D_RNN
D_FF = 3 * D_MODEL
FFN_CONV = 3
LN_EPS = 1e-5

LANES = 128
SUBLANES = 8
NEG_BIG = -1e30

TILE_MIX = 512
TILE_FFN = 512
FF_CHUNK = 512
VMEM_LIMIT = 56 * 1024 * 1024

_GELU_C = math.sqrt(2.0 / math.pi)
_LOG2E = 1.4426950408889634


def _gelu_tanh(x):
    return 0.5 * x * (1.0 + jnp.tanh(_GELU_C * (x + 0.044715 * (x * x * x))))


def _sigmoid(x):
    return 0.5 * jnp.tanh(0.5 * x) + 0.5


def _layer_norm(y, g, b):
    mu = jnp.mean(y, axis=-1, keepdims=True)
    d = y - mu
    var = jnp.mean(d * d, axis=-1, keepdims=True)
    return d * lax.rsqrt(var + LN_EPS) * g + b


def _bdot(a, b):
    return jnp.dot(a, b, preferred_element_type=jnp.float32)


def _w(ref_slice):
    return pltpu.bitcast(ref_slice, jnp.bfloat16)


def _mixer_kernel(alpha, sinks_ref, x_ref, w_in_ref, w_gate_ref, hw_ref, hb_ref, hba_ref,
                  hbx_ref, lam_ref, w_out_ref, g_ref, b_ref, o_ref,
                  q_s, k_s, v_s, xbuf, qbuf, mix_s, hcarry, nk_s):
    T = TILE_MIX
    bf16 = jnp.bfloat16
    s = pl.program_id(1)

    @pl.when(s == 0)
    def _():
        k_s[:, 0:BLOCK, :] = jnp.zeros((ATT_KV_HEADS, BLOCK, LANES), bf16)
        v_s[:, 0:BLOCK, :] = jnp.zeros((ATT_KV_HEADS, BLOCK, LANES), bf16)
        xbuf[0:SUBLANES, :] = jnp.zeros((SUBLANES, D_RNN), jnp.float32)
        qbuf[0:SUBLANES, :] = jnp.zeros((SUBLANES, D_RNN), jnp.float32)
        hcarry[...] = jnp.zeros((SUBLANES, D_RNN), jnp.float32)
        z = -lam_ref[...]
        softplus = jnp.maximum(z, 0.0) + jnp.log1p(jnp.exp(-jnp.abs(z)))
        nk_s[...] = (0.5 * LRU_C) * softplus

    x = x_ref[...]
    xb = x.astype(bf16)
    grp = ATT_HEADS // ATT_KV_HEADS

    uq = _bdot(xb, _w(w_in_ref[:, 0:D_QKV]))
    lane = lax.broadcasted_iota(jnp.int32, (T, LANES), 1)
    lo = lane < HEAD_DIM
    for pair in range(ATT_HEADS // 2):
        q2 = uq[:, pair * LANES:(pair + 1) * LANES] * (HEAD_DIM ** -0.5)
        for half in range(2):
            hh = (2 * pair + half) % grp
            qm = (jnp.where(lo, q2, 0.0) if half == 0 else jnp.where(lo, 0.0, q2)).astype(bf16)
            for n in range(T // BLOCK):
                q_s[n, (2 * pair) // grp, hh * BLOCK:(hh + 1) * BLOCK, :] = qm[n * BLOCK:(n + 1) * BLOCK, :]
    for buf, col in ((k_s, D_ATT), (v_s, D_ATT + D_KV)):
        t = uq[:, col:col + D_KV]
        t_sw = pltpu.roll(t, HEAD_DIM, axis=1)
        buf[0, BLOCK:BLOCK + T, :] = jnp.where(lo, t, t_sw).astype(bf16)
        buf[1, BLOCK:BLOCK + T, :] = jnp.where(lo, t_sw, t).astype(bf16)
    ur = _bdot(xb, _w(w_in_ref[:, D_QKV:D_IN]))
    xbuf[SUBLANES:SUBLANES + T, :] = ur[:, 0:D_RNN]

    qi = lax.broadcasted_iota(jnp.int32, (BLOCK, BLOCK), 0)
    kc = lax.broadcasted_iota(jnp.int32, (BLOCK, BLOCK), 1)
    cur = kc <= qi
    lo_blk = kc < HEAD_DIM
    prev_bias = jnp.where(s == 0, NEG_BIG, 0.0)
    row8 = lax.broadcasted_iota(jnp.int32, (BLOCK, D_RNN), 0) & (SUBLANES - 1)
    hc = hcarry[...]

    for n in range(T // BLOCK):
        r0 = n * BLOCK
        for g in range(ATT_KV_HEADS):
            keys = k_s[g, r0:r0 + 2 * BLOCK, :]
            vals = v_s[g, r0:r0 + 2 * BLOCK, :]
            sc_all = lax.dot_general(q_s[n, g], keys, (((1,), (1,)), ((), ())),
                                     preferred_element_type=jnp.float32)
            ps, rls = [], []
            for hh in range(grp):
                sc = sc_all[hh * BLOCK:(hh + 1) * BLOCK, :]
                prev = sc[:, :BLOCK] + prev_bias if n == 0 else sc[:, :BLOCK]
                sc = jnp.where(cur, sc[:, BLOCK:], prev)
                sink = sinks_ref[g * grp + hh]
                m = jnp.maximum(jnp.max(sc, axis=-1, keepdims=True), sink)
                p = jnp.exp(sc - m)
                rls.append(1.0 / (jnp.sum(p, axis=-1, keepdims=True) + jnp.exp(sink - m)))
                ps.append(jnp.concatenate([jnp.where(cur, 0.0, p), jnp.where(cur, p, 0.0)], axis=1).astype(bf16))
            o_all = _bdot(jnp.concatenate(ps, axis=0), vals)
            for jj in range(grp // 2):
                o_lo = o_all[(2 * jj) * BLOCK:(2 * jj + 1) * BLOCK, :] * rls[2 * jj]
                o_hi = o_all[(2 * jj + 1) * BLOCK:(2 * jj + 2) * BLOCK, :] * rls[2 * jj + 1]
                c0 = (g * (grp // 2) + jj) * LANES
                mix_s[r0:r0 + BLOCK, c0:c0 + LANES] = jnp.where(lo_blk, o_lo, o_hi).astype(bf16)

        xr = xbuf[SUBLANES + r0:SUBLANES + r0 + BLOCK, :]
        x1 = xbuf[SUBLANES - 1 + r0:SUBLANES - 1 + r0 + BLOCK, :]
        qbuf[SUBLANES + r0:SUBLANES + r0 + BLOCK, :] = hw_ref[1:2, :] * xr + hw_ref[0:1, :] * x1
        xh = ((hb_ref[...] + hw_ref[3:4, :] * xr) + hw_ref[2:3, :] * x1
              + qbuf[SUBLANES - 2 + r0:SUBLANES - 2 + r0 + BLOCK, :])
        xhb = xh.astype(bf16)
        ga, gx = [], []
        for c in range(D_RNN // LANES):
            gc = _bdot(xhb[:, c * LANES:(c + 1) * LANES], _w(w_gate_ref[c]))
            ga.append(gc[:, 0:LANES])
            gx.append(gc[:, LANES:2 * LANES])
        th_r = jnp.tanh(jnp.concatenate(ga, axis=1) + hba_ref[...])
        th_i = jnp.tanh(jnp.concatenate(gx, axis=1) + hbx_ref[...])
        nla = nk_s[...] * th_r + nk_s[...]
        a = jnp.exp2(nla * (-_LOG2E))
        y = jnp.tanh(nla) * (a * a + 1.0)
        root = jnp.where(y > 0.0, y * lax.rsqrt(y), 0.0)
        bb = root * (xh * th_i + xh)
        d = 1
        while d < SUBLANES:
            keep = row8 >= d
            a_sh = jnp.where(keep, pltpu.roll(a, d, axis=0), 1.0)
            b_sh = jnp.where(keep, pltpu.roll(bb, d, axis=0), 0.0)
            bb = a * b_sh + bb
            a = a * a_sh
            d *= 2
        hs = []
        for gi in range(BLOCK // SUBLANES):
            g0 = gi * SUBLANES
            hg = a[g0:g0 + SUBLANES, :] * hc + bb[g0:g0 + SUBLANES, :]
            hs.append(hg)
            hc = jnp.broadcast_to(hg[SUBLANES - 1:SUBLANES, :], (SUBLANES, D_RNN))
        gr = ur[r0:r0 + BLOCK, D_RNN:2 * D_RNN]
        mix_s[r0:r0 + BLOCK, D_ATT:D_ATT + D_RNN] = (jnp.concatenate(hs, axis=0) * _gelu_tanh(gr)).astype(bf16)

        if n % 2 == 1:
            h0 = (n - 1) * BLOCK
            yh = alpha * x[h0:h0 + 2 * BLOCK, :] + _bdot(mix_s[h0:h0 + 2 * BLOCK, :], _w(w_out_ref[...]))
            o_ref[h0:h0 + 2 * BLOCK, :] = _layer_norm(yh, g_ref[...], b_ref[...])

    hcarry[...] = hc
    k_s[:, 0:BLOCK, :] = k_s[:, T:T + BLOCK, :]
    v_s[:, 0:BLOCK, :] = v_s[:, T:T + BLOCK, :]
    xbuf[0:SUBLANES, :] = xbuf[T:T + SUBLANES, :]
    qbuf[0:SUBLANES, :] = qbuf[T:T + SUBLANES, :]


def _const_spec(shape):
    nd = len(shape)
    return pl.BlockSpec(shape, lambda b, s: (0,) * nd, pipeline_mode=pl.Buffered(1))


def _mixer(x, alpha, sinks, w_in, w_gate, cw, cb, ba, bx, lam, w_out, g, b):
    B, S, _ = x.shape
    T = TILE_MIX
    assert S % T == 0 and T % (2 * BLOCK) == 0
    row = lambda v: v.reshape(1, -1).astype(jnp.float32)
    tok = pl.BlockSpec((None, T, D_MODEL), lambda b, s: (b, s, 0))
    return pl.pallas_call(
        functools.partial(_mixer_kernel, alpha),
        grid=(B, S // T),
        in_specs=[
            pl.BlockSpec(memory_space=pltpu.SMEM),
            tok,
            _const_spec((D_MODEL // 2, D_IN)),
            _const_spec((D_RNN // LANES, LANES // 2, 2 * LANES)),
            _const_spec((RNN_CONV, D_RNN)),
            _const_spec((1, D_RNN)),
            _const_spec((1, D_RNN)),
            _const_spec((1, D_RNN)),
            _const_spec((1, D_RNN)),
            _const_spec((D_MODEL // 2, D_MODEL)),
            _const_spec((1, D_MODEL)),
            _const_spec((1, D_MODEL)),
        ],
        out_specs=tok,
        out_shape=jax.ShapeDtypeStruct((B, S, D_MODEL), jnp.float32),
        scratch_shapes=[
            pltpu.VMEM((T // BLOCK, ATT_KV_HEADS, (ATT_HEADS // ATT_KV_HEADS) * BLOCK, LANES),
                       jnp.bfloat16),
            pltpu.VMEM((ATT_KV_HEADS, T + BLOCK, LANES), jnp.bfloat16),
            pltpu.VMEM((ATT_KV_HEADS, T + BLOCK, LANES), jnp.bfloat16),
            pltpu.VMEM((T + SUBLANES, D_RNN), jnp.float32),
            pltpu.VMEM((T + SUBLANES, D_RNN), jnp.float32),
            pltpu.VMEM((T, D_MODEL), jnp.bfloat16),
            pltpu.VMEM((SUBLANES, D_RNN), jnp.float32),
            pltpu.VMEM((1, D_RNN), jnp.float32),
        ],
        compiler_params=pltpu.CompilerParams(
            dimension_semantics=("arbitrary", "arbitrary"), vmem_limit_bytes=VMEM_LIMIT),
        name="mixer",
    )(sinks.astype(jnp.float32), x, w_in, w_gate, 0.5 * cw.astype(jnp.float32), row(0.5 * cb),
      row(0.5 * ba), row(0.5 * bx), row(lam), w_out, row(g), row(b))


def _ffn_kernel(alpha, h_ref, p_ref, wup_ref, cw_ref, cb_ref, wdn_ref, wg_ref, bg_ref, wp_ref,
                g_ref, b_ref, o_ref, gbuf, gcarry, acc):
    T = TILE_FFN
    C = FF_CHUNK
    NC = D_FF // C
    bf16 = jnp.bfloat16
    s = pl.program_id(1)

    @pl.when(s == 0)
    def _():
        gcarry[...] = jnp.zeros((SUBLANES, D_FF), jnp.float32)

    h = h_ref[...]
    hb = h.astype(bf16)

    def up(c):
        lo, hi = c * C, (c + 1) * C
        return _bdot(hb, _w(wup_ref[:, lo:hi])), _bdot(hb, _w(wup_ref[:, D_FF + lo:D_FF + hi]))

    gate, val = up(0)
    ple = _sigmoid(_bdot(hb, _w(wg_ref[...])) + bg_ref[...]) * _bdot(p_ref[...].astype(bf16), _w(wp_ref[...]))
    acc[...] = alpha * h + ple
    for c in range(NC):
        lo, hi = c * C, (c + 1) * C
        nxt = up(c + 1) if c + 1 < NC else None
        gb = gbuf.at[c % 2]
        gb[0:SUBLANES, :] = gcarry[:, lo:hi]
        gb[SUBLANES:SUBLANES + T, :] = gate
        gcarry[:, lo:hi] = gate[T - SUBLANES:T, :]
        gc = cb_ref[:, lo:hi] + cw_ref[FFN_CONV - 1:FFN_CONV, lo:hi] * gate
        for j in range(1, FFN_CONV):
            gc = gc + cw_ref[FFN_CONV - 1 - j:FFN_CONV - j, lo:hi] * gb[SUBLANES - j:SUBLANES - j + T, :]
        act = (_gelu_tanh(gc) * val).astype(bf16)
        acc[...] += _bdot(act, _w(wdn_ref[lo // 2:hi // 2, :]))
        if nxt is not None:
            gate, val = nxt
    o_ref[...] = _layer_norm(acc[...], g_ref[...], b_ref[...])


def _ffn(h, p, alpha, wup, cw, cb, wdn, wg, bg, wp, g, b):
    B, S, _ = h.shape
    T = TILE_FFN
    assert S % T == 0 and D_FF % FF_CHUNK == 0
    row = lambda v: v.reshape(1, -1).astype(jnp.float32)
    tok = pl.BlockSpec((None, T, D_MODEL), lambda b, s: (b, s, 0))
    return pl.pallas_call(
        functools.partial(_ffn_kernel, alpha),
        grid=(B, S // T),
        in_specs=[
            tok,
            pl.BlockSpec((None, T, PLE_DIM), lambda b, s: (b, s, 0)),
            _const_spec((D_MODEL // 2, 2 * D_FF)),
            _const_spec((FFN_CONV, D_FF)),
            _const_spec((1, D_FF)),
            _const_spec((D_FF // 2, D_MODEL)),
            _const_spec((D_MODEL // 2, D_MODEL)),
            _const_spec((1, D_MODEL)),
            _const_spec((PLE_DIM // 2, D_MODEL)),
            _const_spec((1, D_MODEL)),
            _const_spec((1, D_MODEL)),
        ],
        out_specs=tok,
        out_shape=jax.ShapeDtypeStruct((B, S, D_MODEL), jnp.float32),
        scratch_shapes=[
            pltpu.VMEM((2, T + SUBLANES, FF_CHUNK), jnp.float32),
            pltpu.VMEM((SUBLANES, D_FF), jnp.float32),
            pltpu.VMEM((T, D_MODEL), jnp.float32),
        ],
        compiler_params=pltpu.CompilerParams(
            dimension_semantics=("arbitrary", "arbitrary"), vmem_limit_bytes=VMEM_LIMIT),
        name="ffn",
    )(h, p, wup, cw.astype(jnp.float32), row(cb), wdn, wg, row(bg), wp, row(g), row(b))


def _pack_rows(w):
    bits = lax.bitcast_convert_type(w.astype(jnp.bfloat16), jnp.uint16).astype(jnp.uint32)
    *lead, k, n = bits.shape
    bits = bits.reshape(*lead, k // 2, 2, n)
    return bits[..., 0, :] | (bits[..., 1, :] << 16)


def _gate_block_diag(wa, wx):
    def bd(w):
        w = w.reshape(D_RNN // LANES, 2, RNN_BLOCK_DIM, RNN_BLOCK_DIM)
        z = jnp.zeros_like(w[:, 0])
        top = jnp.concatenate([w[:, 0], z], axis=2)
        bot = jnp.concatenate([z, w[:, 1]], axis=2)
        return jnp.concatenate([top, bot], axis=1)
    return jnp.concatenate([bd(wa), bd(wx)], axis=2)


def kernel(x, p, w_in, attn_sinks, rnn_conv_w, rnn_conv_b, gate_a_w, gate_a_b, gate_x_w, gate_x_b,
           lru_lambda, w_out, ln1_g, ln1_b, w_ffn_up, ffn_conv_w, ffn_conv_b, w_ffn_down,
           ple_gate_w, ple_gate_b, ple_proj, ln2_g, ln2_b):
    depth = w_in.shape[0]
    alpha = float((2 * depth) ** 0.25)
    h = x
    for l in range(depth):
        h = _mixer(h, alpha, attn_sinks[l], _pack_rows(w_in[l]),
                   _pack_rows(_gate_block_diag(gate_a_w[l], gate_x_w[l])), rnn_conv_w[l], rnn_conv_b[l],
                   gate_a_b[l], gate_x_b[l], lru_lambda[l], _pack_rows(w_out[l]), ln1_g[l], ln1_b[l])
        h = _ffn(h, p[l], alpha, _pack_rows(w_ffn_up[l]), ffn_conv_w[l], ffn_conv_b[l],
                 _pack_rows(w_ffn_down[l]), _pack_rows(ple_gate_w[l]), ple_gate_b[l],
                 _pack_rows(ple_proj[l]), ln2_g[l], ln2_b[l])
    return h
```
